```python
import jax, jax.numpy as jnp
from jax import lax
import numpy as np

D_MODEL = 1024
BATCH = 2
SEQ = 8192
DEPTH = 2

GRID_W = 64
CTX_LEN = 256
N_MIXERS = 2
EXPAND = 2
D_INNER = EXPAND * D_MODEL
LRU_BLOCKS = 16
LRU_BLOCK = D_INNER // LRU_BLOCKS
CONV_W = 4
CONV_LEFT = 2
LRU_C = 8.0
POOL_WINDOWS = (2, 4, 8, 16)
N_POOL_GROUPS = len(POOL_WINDOWS)
POOL_GROUP = D_INNER // N_POOL_GROUPS
N_A_LAYERS = (DEPTH + 1) // 2
N_B_LAYERS = DEPTH // 2
ALPHA = float((2 * DEPTH) ** 0.25)
BETA = float((8 * DEPTH) ** -0.25)
LN_EPS = 1e-5

kernel_name = "hybrid_rglru_pool_deepnorm_prefix"


def _layer_norm(v, g, b):
    vf = v.astype(jnp.float32)
    mu = jnp.mean(vf, axis=-1, keepdims=True)
    var = jnp.mean(jnp.square(vf - mu), axis=-1, keepdims=True)
    y = (vf - mu) * lax.rsqrt(var + LN_EPS) * g.astype(jnp.float32) + b.astype(jnp.float32)
    return y.astype(v.dtype)


def _adaln(cvec, w_mod, b_mod):
    m = jax.nn.silu(cvec) @ w_mod + b_mod
    shift, scale, gate = jnp.split(m, 3, axis=-1)
    return shift, scale, gate


def _centred_dwconv(u, w, b):
    L = u.shape[1]
    up = jnp.pad(u, ((0, 0), (CONV_LEFT, CONV_W - 1 - CONV_LEFT), (0, 0)))
    out = up[:, 0:L] * w[0]
    for k in range(1, CONV_W):
        out = out + up[:, k:k + L] * w[k]
    return out + b


def _lru_coeffs(uf, wa, ba, wx, bx, lam):
    bn, L, _ = uf.shape
    ub = uf.reshape(bn, L, LRU_BLOCKS, LRU_BLOCK)
    r = jax.nn.sigmoid(jnp.einsum('blnh,nhk->blnk', ub, wa.astype(jnp.float32)).reshape(bn, L, D_INNER) + ba.astype(jnp.float32))
    i = jax.nn.sigmoid(jnp.einsum('blnh,nhk->blnk', ub, wx.astype(jnp.float32)).reshape(bn, L, D_INNER) + bx.astype(jnp.float32))
    log_a = LRU_C * r * jax.nn.log_sigmoid(lam.astype(jnp.float32))
    a = jnp.exp(log_a)
    drive = jnp.sqrt(-jnp.expm1(2.0 * log_a)) * (i * uf)
    return a, drive


def _linear_scan(a, b, h0, reverse):
    if h0 is not None:
        if reverse:
            b = b.at[:, -1].add(a[:, -1] * h0)
        else:
            b = b.at[:, 0].add(a[:, 0] * h0)

    def combine(left, right):
        a1, b1 = left
        a2, b2 = right
        return a1 * a2, a2 * b1 + b2

    _, h = lax.associative_scan(combine, (a, b), reverse=reverse, axis=1)
    return h


def _rglru(u, wa, ba, wx, bx, lam, h0_f, h0_b):
    uf = u.astype(jnp.float32)
    af, df = _lru_coeffs(uf, wa[0], ba[0], wx[0], bx[0], lam[0])
    ab, db = _lru_coeffs(uf, wa[1], ba[1], wx[1], bx[1], lam[1])
    h_fwd = _linear_scan(af, df, h0_f, False)
    h_bwd = _linear_scan(ab, db, h0_b, True)
    return h_fwd, h_bwd


def _window_mean(v, w, axis):
    n = v.shape[axis]
    cs = jnp.cumsum(v.astype(jnp.float32), axis=axis)
    pad = [(0, 0)] * v.ndim
    pad[axis] = (1, 0)
    cs = jnp.pad(cs, pad)
    t = np.arange(n)
    lo = np.clip(t - w // 2, 0, n)
    hi = np.clip(t + w // 2, 0, n)
    s = jnp.take(cs, hi, axis=axis) - jnp.take(cs, lo, axis=axis)
    cnt_shape = [1] * v.ndim
    cnt_shape[axis] = n
    cnt = jnp.asarray((hi - lo).astype(np.float32)).reshape(cnt_shape)
    return s / cnt


def _pool_grid(u, w_p, scale, rows):
    bn = u.shape[0]
    ug = u.reshape(bn, rows, GRID_W, D_INNER)
    outs = []
    for k, w in enumerate(POOL_WINDOWS):
        seg = ug[..., k * POOL_GROUP:(k + 1) * POOL_GROUP]
        m = _window_mean(_window_mean(seg, w, 2), w, 1)
        d = (m - seg.astype(jnp.float32)).astype(u.dtype)
        outs.append(jnp.einsum('brcg,gh->brch', d, w_p[k]))
    y = jnp.concatenate(outs, axis=-1).reshape(bn, rows * GRID_W, D_INNER)
    return y * scale


def _pool_seq(u, w_p, scale):
    outs = []
    for k, w in enumerate(POOL_WINDOWS):
        seg = u[..., k * POOL_GROUP:(k + 1) * POOL_GROUP]
        d = (_window_mean(seg, w, 1) - seg.astype(jnp.float32)).astype(u.dtype)
        outs.append(jnp.einsum('blg,gh->blh', d, w_p[k]))
    return jnp.concatenate(outs, axis=-1) * scale


def _ctx_needed_after(i):
    return any(j % N_MIXERS == 0 for j in range(i + 1, DEPTH))


def setup_inputs(seed: int = 0) -> dict:
    key = jax.random.key(seed)
    ks = jax.random.split(key, 20)
    f32 = jnp.float32
    x = jax.random.normal(ks[0], (BATCH, SEQ, D_MODEL), f32)
    c = jax.random.normal(ks[1], (BATCH, D_MODEL), f32)
    ctx = jax.random.normal(ks[2], (BATCH, CTX_LEN, D_MODEL), f32)
    c_ctx = jax.random.normal(ks[3], (D_MODEL,), f32)
    w_mod = jax.random.normal(ks[4], (DEPTH, D_MODEL, 3 * D_MODEL), f32) * (0.5 * D_MODEL ** -0.5)
    b_mod = jax.random.normal(ks[5], (DEPTH, 3 * D_MODEL), f32) * 0.02
    w_in = jax.random.normal(ks[6], (DEPTH, D_MODEL, 2 * D_INNER), f32) * D_MODEL ** -0.5
    w_out = jax.random.normal(ks[7], (DEPTH, D_INNER, D_MODEL), f32) * (D_INNER ** -0.5 * BETA)
    ln_g = 1.0 + 0.02 * jax.random.normal(ks[8], (DEPTH, D_MODEL), f32)
    ln_b = 0.02 * jax.random.normal(ks[9], (DEPTH, D_MODEL), f32)
    conv_w = jax.random.normal(ks[10], (N_A_LAYERS, CONV_W, D_INNER), f32) * CONV_W ** -0.5
    conv_b = 0.02 * jax.random.normal(ks[11], (N_A_LAYERS, D_INNER), f32)
    lru_wa = jax.random.normal(ks[12], (N_A_LAYERS, 2, LRU_BLOCKS, LRU_BLOCK, LRU_BLOCK), f32) * LRU_BLOCK ** -0.5
    lru_ba = 0.02 * jax.random.normal(ks[13], (N_A_LAYERS, 2, D_INNER), f32)
    lru_wx = jax.random.normal(ks[14], (N_A_LAYERS, 2, LRU_BLOCKS, LRU_BLOCK, LRU_BLOCK), f32) * LRU_BLOCK ** -0.5
    lru_bx = 0.02 * jax.random.normal(ks[15], (N_A_LAYERS, 2, D_INNER), f32)
    a_pow_c = jax.random.uniform(ks[16], (N_A_LAYERS, 2, D_INNER), f32, minval=0.9, maxval=0.999)
    a0 = a_pow_c ** (1.0 / LRU_C)
    lru_lam = jnp.log(a0) - jnp.log1p(-a0)
    pool_w = jax.random.normal(ks[17], (N_B_LAYERS, N_POOL_GROUPS, POOL_GROUP, POOL_GROUP), f32) * POOL_GROUP ** -0.5
    pool_scale = 1.0 + 0.02 * jax.random.normal(ks[18], (N_B_LAYERS, D_INNER), f32)
    return {"x": x, "c": c, "ctx": ctx, "c_ctx": c_ctx, "w_mod": w_mod, "b_mod": b_mod,
            "w_in": w_in, "w_out": w_out, "ln_g": ln_g, "ln_b": ln_b,
            "conv_w": conv_w, "conv_b": conv_b, "lru_wa": lru_wa, "lru_ba": lru_ba,
            "lru_wx": lru_wx, "lru_bx": lru_bx, "lru_lam": lru_lam,
            "pool_w": pool_w, "pool_scale": pool_scale}


def reference(x, c, ctx, c_ctx, w_mod, b_mod, w_in, w_out, ln_g, ln_b,
              conv_w, conv_b, lru_wa, lru_ba, lru_wx, lru_bx, lru_lam,
              pool_w, pool_scale):
    rows = x.shape[1] // GRID_W
    xc = ctx
    for i in range(DEPTH):
        kind = i % N_MIXERS
        j = i // N_MIXERS
        ctx_out = _ctx_needed_after(i)
        sh, sc, gt = _adaln(c, w_mod[i], b_mod[i])
        h = x * (1.0 + sc[:, None]) + sh[:, None]
        u, g = jnp.split(h @ w_in[i], 2, axis=-1)
        if kind == 0 or ctx_out:
            shc, scc, gtc = _adaln(c_ctx, w_mod[i], b_mod[i])
            hc = xc * (1.0 + scc) + shc
            if ctx_out:
                uc, gc = jnp.split(hc @ w_in[i], 2, axis=-1)
            else:
                uc = hc @ w_in[i][:, :D_INNER]
        if kind == 0:
            uc = _centred_dwconv(uc, conv_w[j], conv_b[j])
            hcf, hcb = _rglru(uc, lru_wa[j], lru_ba[j], lru_wx[j], lru_bx[j], lru_lam[j], None, None)
            u = _centred_dwconv(u, conv_w[j], conv_b[j])
            hf, hb = _rglru(u, lru_wa[j], lru_ba[j], lru_wx[j], lru_bx[j], lru_lam[j],
                            hcf[:, -1], hcb[:, 0])
            y = (hf + hb).astype(x.dtype)
            if ctx_out:
                yc = (hcf + hcb).astype(xc.dtype)
        else:
            y = _pool_grid(u, pool_w[j], pool_scale[j], rows)
            if ctx_out:
                yc = _pool_seq(uc, pool_w[j], pool_scale[j])
        branch = (y * jax.nn.silu(g)) @ w_out[i]
        x = _layer_norm(ALPHA * x + gt[:, None] * branch, ln_g[i], ln_b[i])
        if ctx_out:
            branch_c = (yc * jax.nn.silu(gc)) @ w_out[i]
            xc = _layer_norm(ALPHA * xc + gtc * branch_c, ln_g[i], ln_b[i])
    return x
```

```python
import functools

import jax
import jax.numpy as jnp
from jax import lax
from jax.experimental import pallas as pl
from jax.experimental.pallas import tpu as pltpu

F32 = jnp.float32
BF16 = jnp.bfloat16

D_MODEL = 1024
D_INNER = 2048
DEPTH = 2
GRID_W = 64
LRU_BLOCK = 128
CONV_W = 4
CONV_LEFT = 2
LRU_C = 8.0
POOL_WINDOWS = (2, 4, 8, 16)
POOL_GROUP = 512
ALPHA = float((2 * DEPTH) ** 0.25)
LN_EPS = 1e-5

LANES = 128
SUBLANES = 8
BF16_ROWS = 16
HALO = 16
VMEM_LIMIT = 56 * 1024 * 1024

TM0 = 1024
C0 = 256
TM1 = 1024
POOL_HALO = 8 * GRID_W


def _sigmoid(x):
    return 0.5 * jnp.tanh(0.5 * x) + 0.5


def _log_sigmoid(x):
    return jnp.minimum(x, 0.0) - jnp.log(1.0 + jnp.exp(-jnp.abs(x)))


def _bcast_row(x, r):
    return jnp.broadcast_to(x[r:r + 1, :], x.shape)


def _scan8(a, b, reverse):
    row = lax.broadcasted_iota(jnp.int32, a.shape, 0)
    for k in (1, 2, 4):
        if reverse:
            a_s = pltpu.roll(a, SUBLANES - k, 0)
            b_s = pltpu.roll(b, SUBLANES - k, 0)
            valid = row < SUBLANES - k
        else:
            a_s = pltpu.roll(a, k, 0)
            b_s = pltpu.roll(b, k, 0)
            valid = row >= k
        b = jnp.where(valid, a * b_s, 0.0) + b
        a = jnp.where(valid, a * a_s, a)
    return a, b


def _lru_block(pre_r, pre_i, u, ba, bx, cl, carry, reverse):
    r = _sigmoid(pre_r + ba)
    i = _sigmoid(pre_i + bx)
    log_a = r * cl
    a = jnp.exp(log_a)
    drive = jnp.sqrt(1.0 - a * a) * (i * u)
    A, B = _scan8(a, drive, reverse)
    h = A * carry + B
    edge = 0 if reverse else SUBLANES - 1
    new_carry = _bcast_row(A, edge) * carry + _bcast_row(B, edge)
    return h, new_carry


def _conv_rows(u_scr, cw_ref, cb_ref, r0, rb):
    win = u_scr[pl.ds(r0, rb + 2 * SUBLANES), :]
    base = SUBLANES - CONV_LEFT
    acc = win[base:base + rb, :] * cw_ref[0:1, :]
    for k in range(1, CONV_W):
        acc = acc + win[base + k:base + k + rb, :] * cw_ref[k:k + 1, :]
    return acc + cb_ref[...]


def _gate_preacts(ucb_ref, wgate_ref, pre_scr, c, ndir):
    for kb in range(c // LRU_BLOCK):
        sl = slice(kb * LRU_BLOCK, (kb + 1) * LRU_BLOCK)
        pre = jnp.dot(ucb_ref[:, sl], wgate_ref[kb], preferred_element_type=F32)
        for q in range(2 * ndir):
            pre_scr[:, q * c + kb * LRU_BLOCK:q * c + (kb + 1) * LRU_BLOCK] = (
                pre[:, q * LRU_BLOCK:(q + 1) * LRU_BLOCK])


def _mod_kernel(cc_ref, w_ref, b_ref, o_ref):
    s = cc_ref[...]
    s = s * _sigmoid(s)
    o_ref[...] = jnp.dot(s, w_ref[...], preferred_element_type=F32,
                         precision=lax.Precision.HIGHEST) + b_ref[...]


def _adaln(cc, w_mod, b_mod):
    nblk = 4
    wn = 3 * D_MODEL // nblk
    return pl.pallas_call(
        _mod_kernel,
        grid=(DEPTH, nblk),
        in_specs=[
            pl.BlockSpec((SUBLANES, D_MODEL), lambda l, j: (0, 0)),
            pl.BlockSpec((None, D_MODEL, wn), lambda l, j: (l, 0, j)),
            pl.BlockSpec((None, 1, wn), lambda l, j: (l, 0, j)),
        ],
        out_specs=pl.BlockSpec((None, SUBLANES, wn), lambda l, j: (l, 0, j)),
        out_shape=jax.ShapeDtypeStruct((DEPTH, SUBLANES, 3 * D_MODEL), F32),
        name="adaln",
    )(cc, w_mod, b_mod.reshape(DEPTH, 1, 3 * D_MODEL))


def _ctx_kernel(x_ref, sh_ref, sc_ref, wu_ref, cw_ref, cb_ref, wgate_ref, ba_ref, bx_ref, lam_ref,
                hf_ref, hb_ref, u_scr, ucb_scr, pre_scr, *, tm, c):
    h = (x_ref[...] * (1.0 + sc_ref[...]) + sh_ref[...]).astype(BF16)
    u_scr[0:SUBLANES, :] = jnp.zeros((SUBLANES, c), F32)
    u_scr[SUBLANES + tm:, :] = jnp.zeros((HALO, c), F32)
    u_scr[SUBLANES:SUBLANES + tm, :] = jnp.dot(h, wu_ref[...], preferred_element_type=F32)

    rb = 32
    def conv_body(j, _):
        r0 = pl.multiple_of(j * rb, rb)
        uc = _conv_rows(u_scr, cw_ref, cb_ref, r0, rb)
        ucb_scr[pl.ds(r0, rb), :] = uc.astype(BF16)
        pre_scr[pl.ds(r0, rb), 4 * c:5 * c] = uc
        return 0
    lax.fori_loop(0, tm // rb, conv_body, 0)
    _gate_preacts(ucb_scr, wgate_ref, pre_scr, c, 2)

    nblk = tm // SUBLANES
    for d in range(2):
        reverse = d == 1
        cl = LRU_C * _log_sigmoid(lam_ref[d:d + 1, :])
        ba = ba_ref[d:d + 1, :]
        bx = bx_ref[d:d + 1, :]

        def body(j, carry, reverse=reverse, cl=cl, ba=ba, bx=bx, d=d):
            jj = nblk - 1 - j if reverse else j
            r0 = pl.multiple_of(jj * SUBLANES, SUBLANES)
            rows = pl.ds(r0, SUBLANES)
            _, carry = _lru_block(pre_scr[rows, (2 * d) * c:(2 * d + 1) * c],
                                  pre_scr[rows, (2 * d + 1) * c:(2 * d + 2) * c],
                                  pre_scr[rows, 4 * c:5 * c], ba, bx, cl, carry, reverse)
            return carry

        final = lax.fori_loop(0, nblk, body, jnp.zeros((SUBLANES, c), F32))
        if reverse:
            hb_ref[...] = final
        else:
            hf_ref[...] = final


def _ctx_states(ctx, mod4, wu, conv_w, conv_b, wgate, ba, bx, lam):
    bsz, tm, _ = ctx.shape
    c = C0
    nc = D_INNER // c
    kern = functools.partial(_ctx_kernel, tm=tm, c=c)
    vec = lambda: pl.BlockSpec((2, c), lambda b, j: (0, j))
    state = jax.ShapeDtypeStruct((bsz, SUBLANES, D_INNER), F32)
    return pl.pallas_call(
        kern,
        grid=(bsz, nc),
        in_specs=[
            pl.BlockSpec((None, tm, D_MODEL), lambda b, j: (b, 0, 0)),
            pl.BlockSpec((None, None, 1, D_MODEL), lambda b, j: (0, 2, 0, 0)),
            pl.BlockSpec((None, None, 1, D_MODEL), lambda b, j: (0, 2, 0, 1)),
            pl.BlockSpec((D_MODEL, c), lambda b, j: (0, j)),
            pl.BlockSpec((CONV_W, c), lambda b, j: (0, j)),
            pl.BlockSpec((1, c), lambda b, j: (0, j)),
            pl.BlockSpec((c // LRU_BLOCK, LRU_BLOCK, 4 * LRU_BLOCK), lambda b, j: (j, 0, 0)),
            vec(), vec(), vec(),
        ],
        out_specs=[pl.BlockSpec((None, SUBLANES, c), lambda b, j: (b, 0, j))] * 2,
        out_shape=[state, state],
        scratch_shapes=[
            pltpu.VMEM((SUBLANES + tm + HALO, c), F32),
            pltpu.VMEM((tm, c), BF16),
            pltpu.VMEM((tm, 5 * c), F32),
        ],
        compiler_params=pltpu.CompilerParams(
            dimension_semantics=("arbitrary", "arbitrary"), vmem_limit_bytes=VMEM_LIMIT),
        name="ctx_states",
    )(ctx, mod4, mod4, wu, conv_w, conv_b, wgate, ba, bx, lam)


def _l0_fwd_kernel(x_ref, xn_ref, sh_ref, sc_ref, wu_ref, wg_ref, cw_ref, cb_ref, wgate_ref,
                   ba_ref, bx_ref, lam_ref, h0_ref,
                   uc_out, sg_out, hf_out,
                   h_scr, u_scr, ucb_scr, pre_scr, prev_scr, carry_scr, *, tm, c):
    i = pl.program_id(1)
    j = pl.program_id(2)
    nt = pl.num_programs(1)

    @pl.when(j == 0)
    def _():
        def body(k, _):
            r0 = pl.multiple_of(k * BF16_ROWS, BF16_ROWS)
            rows = pl.ds(r0, BF16_ROWS)
            h_scr[rows, :] = (x_ref[rows, :] * (1.0 + sc_ref[...]) + sh_ref[...]).astype(BF16)
            return 0
        lax.fori_loop(0, tm // BF16_ROWS, body, 0)
        h_scr[tm:tm + HALO, :] = (xn_ref[...] * (1.0 + sc_ref[...]) + sh_ref[...]).astype(BF16)

    u_scr[SUBLANES:, :] = jnp.dot(h_scr[...], wu_ref[...], preferred_element_type=F32)
    g = jnp.dot(h_scr[0:tm, :], wg_ref[...], preferred_element_type=F32)
    sg_out[...] = g * _sigmoid(g)

    @pl.when(i == 0)
    def _():
        u_scr[0:SUBLANES, :] = jnp.zeros((SUBLANES, c), F32)

    @pl.when(i > 0)
    def _():
        u_scr[0:SUBLANES, :] = prev_scr[j]

    @pl.when(i == nt - 1)
    def _():
        u_scr[SUBLANES + tm:, :] = jnp.zeros((HALO, c), F32)

    prev_scr[j] = u_scr[tm:tm + SUBLANES, :]

    rb = 32
    def conv_body(k, _):
        r0 = pl.multiple_of(k * rb, rb)
        uc = _conv_rows(u_scr, cw_ref, cb_ref, r0, rb)
        uc_out[pl.ds(r0, rb), :] = uc
        ucb_scr[pl.ds(r0, rb), :] = uc.astype(BF16)
        return 0
    lax.fori_loop(0, tm // rb, conv_body, 0)

    _gate_preacts(ucb_scr, wgate_ref, pre_scr, c, 1)

    cl = LRU_C * _log_sigmoid(lam_ref[...])

    def body(k, carry):
        r0 = pl.multiple_of(k * SUBLANES, SUBLANES)
        rows = pl.ds(r0, SUBLANES)
        h, carry = _lru_block(pre_scr[rows, 0:c], pre_scr[rows, c:2 * c], uc_out[rows, :],
                              ba_ref[...], bx_ref[...], cl, carry, False)
        hf_out[rows, :] = h
        return carry

    @pl.when(i == 0)
    def _():
        carry_scr[j] = h0_ref[...]

    carry_scr[j] = lax.fori_loop(0, tm // SUBLANES, body, carry_scr[j], unroll=2)


def _l0_forward(x, mod4, wu, wg, conv_w, conv_b, wgate, ba, bx, lam, h0):
    bsz, seq, _ = x.shape
    tm, c = TM0, C0
    nt, nc = seq // tm, D_INNER // c
    last_halo = seq // HALO - 1
    kern = functools.partial(_l0_fwd_kernel, tm=tm, c=c)
    act = jax.ShapeDtypeStruct((bsz, seq, D_INNER), F32)
    chunk = lambda: pl.BlockSpec((1, c), lambda b, i, j: (0, j))
    tile = lambda: pl.BlockSpec((None, tm, c), lambda b, i, j: (b, i, j))
    return pl.pallas_call(
        kern,
        grid=(bsz, nt, nc),
        in_specs=[
            pl.BlockSpec((None, tm, D_MODEL), lambda b, i, j: (b, i, 0)),
            pl.BlockSpec((None, HALO, D_MODEL),
                         lambda b, i, j: (b, jnp.minimum((i + 1) * (tm // HALO), last_halo), 0)),
            pl.BlockSpec((None, None, 1, D_MODEL), lambda b, i, j: (0, b, 0, 0)),
            pl.BlockSpec((None, None, 1, D_MODEL), lambda b, i, j: (0, b, 0, 1)),
            pl.BlockSpec((D_MODEL, c), lambda b, i, j: (0, j)),
            pl.BlockSpec((D_MODEL, c), lambda b, i, j: (0, j)),
            pl.BlockSpec((CONV_W, c), lambda b, i, j: (0, j)),
            chunk(),
            pl.BlockSpec((c // LRU_BLOCK, LRU_BLOCK, 2 * LRU_BLOCK), lambda b, i, j: (j, 0, 0)),
            chunk(), chunk(), chunk(),
            pl.BlockSpec((None, SUBLANES, c), lambda b, i, j: (b, 0, j)),
        ],
        out_specs=[tile(), tile(), tile()],
        out_shape=[act, act, act],
        scratch_shapes=[
            pltpu.VMEM((tm + HALO, D_MODEL), BF16),
            pltpu.VMEM((SUBLANES + tm + HALO, c), F32),
            pltpu.VMEM((tm, c), BF16),
            pltpu.VMEM((tm, 2 * c), F32),
            pltpu.VMEM((nc, SUBLANES, c), F32),
            pltpu.VMEM((nc, SUBLANES, c), F32),
        ],
        compiler_params=pltpu.CompilerParams(
            dimension_semantics=("arbitrary", "arbitrary", "arbitrary"),
            vmem_limit_bytes=VMEM_LIMIT),
        name="l0_forward",
    )(x, x, mod4, mod4, wu, wg, conv_w, conv_b, wgate, ba, bx, lam, h0)


def _residual_ln_rows(x_ref, acc_scr, gt_ref, lng_ref, lnb_ref, o_ref, tm):
    def body(k, _):
        r0 = pl.multiple_of(k * SUBLANES, SUBLANES)
        rows = pl.ds(r0, SUBLANES)
        v = ALPHA * x_ref[rows, :] + gt_ref[...] * acc_scr[rows, :]
        mu = jnp.mean(v, axis=-1, keepdims=True)
        d = v - mu
        var = jnp.mean(d * d, axis=-1, keepdims=True)
        o_ref[rows, :] = d * lax.rsqrt(var + LN_EPS) * lng_ref[...] + lnb_ref[...]
        return 0
    lax.fori_loop(0, tm // SUBLANES, body, 0, unroll=2)


def _l0_bwd_kernel(uc_ref, sg_ref, hf_ref, x_ref, gt_ref, wgate_ref, ba_ref, bx_ref, lam_ref,
                   h0_ref, wo_ref, lng_ref, lnb_ref,
                   o_ref,
                   pre_scr, yg_scr, acc_scr, carry_scr, *, tm, c):
    i = pl.program_id(1)
    j = pl.program_id(2)
    nc = pl.num_programs(2)

    for kb in range(c // LRU_BLOCK):
        sl = slice(kb * LRU_BLOCK, (kb + 1) * LRU_BLOCK)
        pre = jnp.dot(uc_ref[:, sl].astype(BF16), wgate_ref[kb], preferred_element_type=F32)
        pre_scr[:, sl] = pre[:, 0:LRU_BLOCK]
        pre_scr[:, c + kb * LRU_BLOCK:c + (kb + 1) * LRU_BLOCK] = pre[:, LRU_BLOCK:]

    cl = LRU_C * _log_sigmoid(lam_ref[...])
    nblk = tm // BF16_ROWS

    def body(k, carry):
        r0 = pl.multiple_of((nblk - 1 - k) * BF16_ROWS, BF16_ROWS)
        ys = [None, None]
        for half in (1, 0):
            rows = pl.ds(r0 + half * SUBLANES, SUBLANES)
            h, carry = _lru_block(pre_scr[rows, 0:c], pre_scr[rows, c:2 * c], uc_ref[rows, :],
                                  ba_ref[...], bx_ref[...], cl, carry, True)
            ys[half] = (hf_ref[rows, :] + h) * sg_ref[rows, :]
        yg_scr[pl.ds(r0, BF16_ROWS), :] = jnp.concatenate(ys, axis=0).astype(BF16)
        return carry

    @pl.when(i == 0)
    def _():
        carry_scr[j] = h0_ref[...]

    carry_scr[j] = lax.fori_loop(0, nblk, body, carry_scr[j])

    @pl.when(j == 0)
    def _():
        acc_scr[...] = jnp.zeros_like(acc_scr)

    acc_scr[...] += jnp.dot(yg_scr[...], wo_ref[...], preferred_element_type=F32)

    @pl.when(j == nc - 1)
    def _():
        _residual_ln_rows(x_ref, acc_scr, gt_ref, lng_ref, lnb_ref, o_ref, tm)


def _l0_backward(uc, sg, hf, x, mod4, wgate, ba, bx, lam, h0, wo, ln_g, ln_b):
    bsz, seq, _ = x.shape
    tm, c = TM0, C0
    nt, nc = seq // tm, D_INNER // c
    kern = functools.partial(_l0_bwd_kernel, tm=tm, c=c)
    chunk = lambda: pl.BlockSpec((1, c), lambda b, i, j: (0, j))
    tile = lambda: pl.BlockSpec((None, tm, c), lambda b, i, j: (b, nt - 1 - i, j))
    full = lambda: pl.BlockSpec((1, D_MODEL), lambda b, i, j: (0, 0))
    return pl.pallas_call(
        kern,
        grid=(bsz, nt, nc),
        in_specs=[
            tile(), tile(), tile(),
            pl.BlockSpec((None, tm, D_MODEL), lambda b, i, j: (b, nt - 1 - i, 0)),
            pl.BlockSpec((None, None, 1, D_MODEL), lambda b, i, j: (0, b, 0, 2)),
            pl.BlockSpec((c // LRU_BLOCK, LRU_BLOCK, 2 * LRU_BLOCK), lambda b, i, j: (j, 0, 0)),
            chunk(), chunk(), chunk(),
            pl.BlockSpec((None, SUBLANES, c), lambda b, i, j: (b, 0, j)),
            pl.BlockSpec((c, D_MODEL), lambda b, i, j: (j, 0)),
            full(), full(),
        ],
        out_specs=pl.BlockSpec((None, tm, D_MODEL), lambda b, i, j: (b, nt - 1 - i, 0)),
        out_shape=jax.ShapeDtypeStruct((bsz, seq, D_MODEL), F32),
        scratch_shapes=[
            pltpu.VMEM((tm, 2 * c), F32),
            pltpu.VMEM((tm, c), BF16),
            pltpu.VMEM((tm, D_MODEL), F32),
            pltpu.VMEM((nc, SUBLANES, c), F32),
        ],
        compiler_params=pltpu.CompilerParams(
            dimension_semantics=("arbitrary", "arbitrary", "arbitrary"),
            vmem_limit_bytes=VMEM_LIMIT),
        name="l0_backward",
    )(uc, sg, hf, x, mod4, wgate, ba, bx, lam, h0, wo, ln_g, ln_b)


def _l1_in_kernel(x_ref, sh_ref, sc_ref, wu_ref, wg_ref, u_out, sg_out, h_scr, *, tm):
    @pl.when(pl.program_id(2) == 0)
    def _():
        def body(k, _):
            rows = pl.ds(pl.multiple_of(k * BF16_ROWS, BF16_ROWS), BF16_ROWS)
            h_scr[rows, :] = (x_ref[rows, :] * (1.0 + sc_ref[...]) + sh_ref[...]).astype(BF16)
            return 0
        lax.fori_loop(0, tm // BF16_ROWS, body, 0)

    u_out[...] = jnp.dot(h_scr[...], wu_ref[...], preferred_element_type=F32)
    g = jnp.dot(h_scr[...], wg_ref[...], preferred_element_type=F32)
    sg_out[...] = g * _sigmoid(g)


def _l1_inproj(x, mod4, wu, wg):
    bsz, seq, _ = x.shape
    tm, c = TM1, POOL_GROUP
    nt, nc = seq // tm, D_INNER // c
    act = jax.ShapeDtypeStruct((bsz, seq, D_INNER), F32)
    tile = lambda: pl.BlockSpec((None, tm, c), lambda b, i, j: (b, i, j))
    return pl.pallas_call(
        functools.partial(_l1_in_kernel, tm=tm),
        grid=(bsz, nt, nc),
        in_specs=[
            pl.BlockSpec((None, tm, D_MODEL), lambda b, i, j: (b, i, 0)),
            pl.BlockSpec((None, None, 1, D_MODEL), lambda b, i, j: (1, b, 0, 0)),
            pl.BlockSpec((None, None, 1, D_MODEL), lambda b, i, j: (1, b, 0, 1)),
            pl.BlockSpec((D_MODEL, c), lambda b, i, j: (0, j)),
            pl.BlockSpec((D_MODEL, c), lambda b, i, j: (0, j)),
        ],
        out_specs=[tile(), tile()],
        out_shape=[act, act],
        scratch_shapes=[pltpu.VMEM((tm, D_MODEL), BF16)],
        compiler_params=pltpu.CompilerParams(
            dimension_semantics=("arbitrary", "arbitrary", "arbitrary"),
            vmem_limit_bytes=VMEM_LIMIT),
        name="l1_inproj",
    )(x, mod4, mod4, wu, wg)


def _shift_rows(x, s):
    n = x.shape[0]
    t = lax.broadcasted_iota(jnp.int32, x.shape, 0)
    rolled = pltpu.roll(x, s % n, 0)
    valid = (t >= s) if s > 0 else (t < n + s)
    return jnp.where(valid, rolled, 0.0)


def _col_window_sum(x, w):
    half = w // 2
    right = x
    left = _shift_rows(x, 1)
    m = 1
    while m < half:
        right = right + _shift_rows(right, -m)
        left = left + _shift_rows(left, m)
        m *= 2
    return left + right


def _window_count(pos, half, n):
    return (jnp.minimum(pos + half, n) - jnp.maximum(pos - half, 0)).astype(F32)


def _pool_group(e_scr, d_scr, w, row0, tm, n_rows):
    half = w // 2
    halo = POOL_HALO
    col = lax.broadcasted_iota(jnp.int32, (GRID_W, LANES), 0)
    inv_c = 1.0 / _window_count(col, half, GRID_W)
    nrow_tile = tm // GRID_W

    def body(r, _):
        base = pl.multiple_of(halo + r * GRID_W, GRID_W)
        grow = jnp.zeros((GRID_W, LANES), jnp.int32) + (row0 + r)
        inv = inv_c / _window_count(grow, half, n_rows)
        for lt in range(POOL_GROUP // LANES):
            lanes = slice(lt * LANES, (lt + 1) * LANES)
            s = e_scr[pl.ds(pl.multiple_of(base - half * GRID_W, GRID_W), GRID_W), lanes]
            for dr in range(-half + 1, half):
                s = s + e_scr[pl.ds(pl.multiple_of(base + dr * GRID_W, GRID_W), GRID_W), lanes]
            s = _col_window_sum(s, w)
            d = s * inv - e_scr[pl.ds(base, GRID_W), lanes]
            d_scr[pl.ds(pl.multiple_of(r * GRID_W, GRID_W), GRID_W), lanes] = d.astype(BF16)
        return 0

    lax.fori_loop(0, nrow_tile, body, 0)


def _l1_pool_kernel(up_ref, u_ref, un_ref, sg_ref, x_ref, gt_ref, pw_ref, ps_ref, wo_ref,
                    lng_ref, lnb_ref, o_ref, e_scr, d_scr, acc_scr, *, tm, n_rows):
    i = pl.program_id(1)
    k = pl.program_id(2)
    nt = pl.num_programs(1)
    halo = POOL_HALO

    e_scr[halo:halo + tm, :] = u_ref[...]

    @pl.when(i == 0)
    def _():
        e_scr[0:halo, :] = jnp.zeros((halo, POOL_GROUP), F32)

    @pl.when(i > 0)
    def _():
        e_scr[0:halo, :] = up_ref[...]

    @pl.when(i == nt - 1)
    def _():
        e_scr[halo + tm:, :] = jnp.zeros((halo, POOL_GROUP), F32)

    @pl.when(i < nt - 1)
    def _():
        e_scr[halo + tm:, :] = un_ref[...]

    row0 = i * (tm // GRID_W)
    for g, w in enumerate(POOL_WINDOWS):
        @pl.when(k == g)
        def _(w=w):
            _pool_group(e_scr, d_scr, w, row0, tm, n_rows)

    y = jnp.dot(d_scr[...], pw_ref[...], preferred_element_type=F32)
    yg = (y * ps_ref[...] * sg_ref[...]).astype(BF16)

    @pl.when(k == 0)
    def _():
        acc_scr[...] = jnp.zeros_like(acc_scr)

    acc_scr[...] += jnp.dot(yg, wo_ref[...], preferred_element_type=F32)

    @pl.when(k == pl.num_programs(2) - 1)
    def _():
        _residual_ln_rows(x_ref, acc_scr, gt_ref, lng_ref, lnb_ref, o_ref, tm)


def _l1_pool(u, sg, x, mod4, pool_w, pool_scale, wo, ln_g, ln_b):
    bsz, seq, _ = x.shape
    tm, c = TM1, POOL_GROUP
    nt, ng = seq // tm, D_INNER // c
    hb = tm // POOL_HALO
    n_halo = seq // POOL_HALO
    kern = functools.partial(_l1_pool_kernel, tm=tm, n_rows=seq // GRID_W)
    full = lambda: pl.BlockSpec((1, D_MODEL), lambda b, i, k: (0, 0))
    return pl.pallas_call(
        kern,
        grid=(bsz, nt, ng),
        in_specs=[
            pl.BlockSpec((None, POOL_HALO, c), lambda b, i, k: (b, jnp.maximum(i * hb - 1, 0), k)),
            pl.BlockSpec((None, tm, c), lambda b, i, k: (b, i, k)),
            pl.BlockSpec((None, POOL_HALO, c),
                         lambda b, i, k: (b, jnp.minimum((i + 1) * hb, n_halo - 1), k)),
            pl.BlockSpec((None, tm, c), lambda b, i, k: (b, i, k)),
            pl.BlockSpec((None, tm, D_MODEL), lambda b, i, k: (b, i, 0)),
            pl.BlockSpec((None, None, 1, D_MODEL), lambda b, i, k: (1, b, 0, 2)),
            pl.BlockSpec((None, c, c), lambda b, i, k: (k, 0, 0)),
            pl.BlockSpec((1, c), lambda b, i, k: (0, k)),
            pl.BlockSpec((c, D_MODEL), lambda b, i, k: (k, 0)),
            full(), full(),
        ],
        out_specs=pl.BlockSpec((None, tm, D_MODEL), lambda b, i, k: (b, i, 0)),
        out_shape=jax.ShapeDtypeStruct((bsz, seq, D_MODEL), F32),
        scratch_shapes=[
            pltpu.VMEM((tm + 2 * POOL_HALO, c), F32),
            pltpu.VMEM((tm, c), BF16),
            pltpu.VMEM((tm, D_MODEL), F32),
        ],
        compiler_params=pltpu.CompilerParams(
            dimension_semantics=("arbitrary", "arbitrary", "arbitrary"),
            vmem_limit_bytes=VMEM_LIMIT),
        name="l1_pool",
    )(u, u, u, sg, x, mod4, pool_w, pool_scale, wo, ln_g, ln_b)


def kernel(x, c, ctx, c_ctx, w_mod, b_mod, w_in, w_out, ln_g, ln_b, conv_w, conv_b,
           lru_wa, lru_ba, lru_wx, lru_bx, lru_lam, pool_w, pool_scale):
    bsz = x.shape[0]
    cc = jnp.zeros((SUBLANES, D_MODEL), F32).at[:bsz].set(c).at[bsz].set(c_ctx)
    mod = _adaln(cc, w_mod, b_mod)
    mod4 = mod.reshape(DEPTH, SUBLANES, 1, 3 * D_MODEL)
    assert bsz == 2

    w_in_b = w_in.astype(BF16)
    w_out_b = w_out.astype(BF16)
    wu0, wg0 = w_in_b[0, :, :D_INNER], w_in_b[0, :, D_INNER:]
    wu1, wg1 = w_in_b[1, :, :D_INNER], w_in_b[1, :, D_INNER:]

    wa, wx = lru_wa[0].astype(BF16), lru_wx[0].astype(BF16)
    wgate_f = jnp.concatenate([wa[0], wx[0]], axis=-1)
    wgate_b = jnp.concatenate([wa[1], wx[1]], axis=-1)
    wgate_fb = jnp.concatenate([wgate_f, wgate_b], axis=-1)
    ba, bx, lam = lru_ba[0], lru_bx[0], lru_lam[0]

    h0f, h0b = _ctx_states(ctx, mod4, wu0, conv_w[0], conv_b, wgate_fb, ba, bx, lam)
    uc, sg, hf = _l0_forward(x, mod4, wu0, wg0, conv_w[0], conv_b, wgate_f,
                             ba[0:1], bx[0:1], lam[0:1], h0f)
    x1 = _l0_backward(uc, sg, hf, x, mod4, wgate_b, ba[1:2], bx[1:2], lam[1:2], h0b,
                      w_out_b[0], ln_g[0:1], ln_b[0:1])

    u1, sg1 = _l1_inproj(x1, mod4, wu1, wg1)
    return _l1_pool(u1, sg1, x1, mod4, pool_w[0].astype(BF16), pool_scale, w_out_b[1],
                    ln_g[1:2], ln_b[1:2])
```

```python
import functools
import math

import jax
import jax.numpy as jnp
from jax import lax
from jax.experimental import pallas as pl
from jax.experimental.pallas import tpu as pltpu

F32 = jnp.float32
BF16 = jnp.bfloat16

D_MODEL = 1024
D_INNER = 2048
DEPTH = 2
GRID_W = 64
LRU_BLOCK = 128
CONV_W = 4
CONV_LEFT = 2
LRU_C = 8.0
POOL_WINDOWS = (2, 4, 8, 16)
POOL_GROUP = 512
ALPHA = float((2 * DEPTH) ** 0.25)
LN_EPS = 1e-5

LANES = 128
SUBLANES = 8
BF16_ROWS = 16
HALO = 16
GHOST = 2 * SUBLANES
VMEM_LIMIT = 56 * 1024 * 1024

TM0 = 1024
C0 = 256
SCAN_UNROLL = 8
TM1 = 1024
POOL_HALO = 8 * GRID_W


def _log_sigmoid(x):
    return jnp.minimum(x, 0.0) - jnp.log(1.0 + jnp.exp(-jnp.abs(x)))


def _bcast_row(x, r):
    return jnp.broadcast_to(x[r:r + 1, :], x.shape)


def _half_silu(gh):
    return gh * jnp.tanh(gh) + gh


def _decay_rate(lam):
    return (0.5 * LRU_C * math.log2(math.e)) * _log_sigmoid(lam)


def _lru_coeffs(pre_r, pre_i, u, ba_h, bx_h, k1):
    tr = jnp.tanh(pre_r + ba_h)
    ti = jnp.tanh(pre_i + bx_h)
    a = jnp.exp2(tr * k1 + k1)
    v = 1.0 - a * a
    m = jnp.where(v > 0.0, v * lax.rsqrt(v), 0.0)
    uh = 0.5 * u
    return a, m * (uh * ti + uh)


def _fill_ghost_rows(u_scr, tm, prev2, nxt):
    c = u_scr.shape[1]
    row = lax.broadcasted_iota(jnp.int32, (SUBLANES, c), 0)
    for g in range(2):
        src = u_scr[tm + SUBLANES * g:tm + SUBLANES * (g + 1), :]
        pv = prev2[SUBLANES * g:SUBLANES * (g + 1), :]
        u_scr[SUBLANES * g:SUBLANES * (g + 1), :] = jnp.where(
            row > 0, pltpu.roll(src, 1, 0), _bcast_row(pv, SUBLANES - 1))
    first = u_scr[GHOST:GHOST + SUBLANES, :]
    u_scr[GHOST + tm:GHOST + tm + SUBLANES, :] = jnp.where(
        row < SUBLANES - 1, pltpu.roll(first, SUBLANES - 1, 0), _bcast_row(nxt, 0))


def _conv_rows(u_scr, cw_ref, cb_ref, r0, rb):
    acc = u_scr[pl.ds(r0, rb), :] * cw_ref[0:1, :]
    for k in range(1, CONV_W):
        acc = acc + u_scr[pl.ds(r0 + SUBLANES * k, rb), :] * cw_ref[k:k + 1, :]
    return acc + cb_ref[...]


def _gate_preacts(uc_cols, wgate_ref, pre_scr, c, ndir):
    for kb in range(c // LRU_BLOCK):
        pre = jnp.dot(uc_cols(kb), wgate_ref[kb], preferred_element_type=F32)
        for q in range(2 * ndir):
            pre_scr[:, q * c + kb * LRU_BLOCK:q * c + (kb + 1) * LRU_BLOCK] = (
                pre[:, q * LRU_BLOCK:(q + 1) * LRU_BLOCK])


def _local_scan(pre_scr, u_ref, ucol, ba_h, bx_h, k1, col_r, col_i, ph_scr, tm, c, reverse,
                unroll):
    nblk = tm // SUBLANES

    def body(k, carry):
        p, h = carry
        jj = nblk - 1 - k if reverse else k
        rows = pl.ds(pl.multiple_of(jj * SUBLANES, SUBLANES), SUBLANES)
        a, b = _lru_coeffs(pre_scr[rows, col_r:col_r + c], pre_scr[rows, col_i:col_i + c],
                           u_ref[rows, ucol:ucol + c], ba_h, bx_h, k1)
        p = a * p
        h = a * h + b
        if ph_scr is not None:
            ph_scr[rows, 0:c] = p
            ph_scr[rows, c:2 * c] = h
        return p, h

    init = (jnp.ones((SUBLANES, c), F32), jnp.zeros((SUBLANES, c), F32))
    return lax.fori_loop(0, nblk, body, init, unroll=unroll)


def _chain_starts(p_end, h_end, carry, reverse):
    row = lax.broadcasted_iota(jnp.int32, carry.shape, 0)
    start = jnp.zeros_like(carry)
    cur = carry
    order = range(SUBLANES - 1, -1, -1) if reverse else range(SUBLANES)
    for s in order:
        start = jnp.where(row == s, cur, start)
        cur = _bcast_row(h_end, s) + _bcast_row(p_end, s) * cur
    return start, cur


def _mod_kernel(cc_ref, w_ref, b_ref, o_ref):
    s = cc_ref[...]
    s = _half_silu(0.5 * s)
    o_ref[...] = jnp.dot(s, w_ref[...], preferred_element_type=F32,
                         precision=lax.Precision.HIGHEST) + b_ref[...]


def _adaln(cc, w_mod, b_mod):
    nblk = 4
    wn = 3 * D_MODEL // nblk
    return pl.pallas_call(
        _mod_kernel,
        grid=(DEPTH, nblk),
        in_specs=[
            pl.BlockSpec((SUBLANES, D_MODEL), lambda l, j: (0, 0)),
            pl.BlockSpec((None, D_MODEL, wn), lambda l, j: (l, 0, j)),
            pl.BlockSpec((None, 1, wn), lambda l, j: (l, 0, j)),
        ],
        out_specs=pl.BlockSpec((None, SUBLANES, wn), lambda l, j: (l, 0, j)),
        out_shape=jax.ShapeDtypeStruct((DEPTH, SUBLANES, 3 * D_MODEL), F32),
        name="adaln",
    )(cc, w_mod, b_mod.reshape(DEPTH, 1, 3 * D_MODEL))


def _ctx_kernel(x_ref, sh_ref, sc_ref, wu_ref, cw_ref, cb_ref, wgate_ref, ba_ref, bx_ref, lam_ref,
                hf_ref, hb_ref, u_scr, ucb_scr, pre_scr, *, tm, c):
    h = (x_ref[...] * (1.0 + sc_ref[...]) + sh_ref[...]).astype(BF16)
    u_scr[GHOST:GHOST + tm, :] = jnp.dot(h, wu_ref[...], preferred_element_type=F32)
    _fill_ghost_rows(u_scr, tm, jnp.zeros((GHOST, c), F32), jnp.zeros((SUBLANES, c), F32))

    rb = 32
    def conv_body(j, _):
        r0 = pl.multiple_of(j * rb, rb)
        uc = _conv_rows(u_scr, cw_ref, cb_ref, r0, rb)
        ucb_scr[pl.ds(r0, rb), :] = uc.astype(BF16)
        pre_scr[pl.ds(r0, rb), 4 * c:5 * c] = uc
        return 0
    lax.fori_loop(0, tm // rb, conv_body, 0)
    _gate_preacts(lambda kb: ucb_scr[:, kb * LRU_BLOCK:(kb + 1) * LRU_BLOCK],
                  wgate_ref, pre_scr, c, 2)

    for d, out_ref in enumerate((hf_ref, hb_ref)):
        k1 = _decay_rate(lam_ref[d:d + 1, :])
        p_end, h_end = _local_scan(pre_scr, pre_scr, 4 * c, ba_ref[d:d + 1, :], bx_ref[d:d + 1, :],
                                   k1, 2 * d * c, (2 * d + 1) * c, None, tm, c, d == 1,
                                   SCAN_UNROLL)
        _, final = _chain_starts(p_end, h_end, jnp.zeros((SUBLANES, c), F32), d == 1)
        out_ref[...] = final


def _ctx_states(ctx, mod4, wu, conv_w, conv_b, wgate, ba, bx, lam):
    bsz, tm, _ = ctx.shape
    c = C0
    nc = D_INNER // c
    kern = functools.partial(_ctx_kernel, tm=tm, c=c)
    vec = lambda: pl.BlockSpec((2, c), lambda b, j: (0, j))
    state = jax.ShapeDtypeStruct((bsz, SUBLANES, D_INNER), F32)
    return pl.pallas_call(
        kern,
        grid=(bsz, nc),
        in_specs=[
            pl.BlockSpec((None, tm, D_MODEL), lambda b, j: (b, 0, 0)),
            pl.BlockSpec((None, None, 1, D_MODEL), lambda b, j: (0, 2, 0, 0)),
            pl.BlockSpec((None, None, 1, D_MODEL), lambda b, j: (0, 2, 0, 1)),
            pl.BlockSpec((D_MODEL, c), lambda b, j: (0, j)),
            pl.BlockSpec((CONV_W, c), lambda b, j: (0, j)),
            pl.BlockSpec((1, c), lambda b, j: (0, j)),
            pl.BlockSpec((c // LRU_BLOCK, LRU_BLOCK, 4 * LRU_BLOCK), lambda b, j: (j, 0, 0)),
            vec(), vec(), vec(),
        ],
        out_specs=[pl.BlockSpec((None, SUBLANES, c), lambda b, j: (b, 0, j))] * 2,
        out_shape=[state, state],
        scratch_shapes=[
            pltpu.VMEM((GHOST + tm + SUBLANES, c), F32),
            pltpu.VMEM((tm, c), BF16),
            pltpu.VMEM((tm, 5 * c), F32),
        ],
        compiler_params=pltpu.CompilerParams(
            dimension_semantics=("arbitrary", "arbitrary"), vmem_limit_bytes=VMEM_LIMIT),
        name="ctx_states",
    )(ctx, mod4, mod4, wu, conv_w, conv_b, wgate, ba, bx, lam)


def _l0_fwd_kernel(x_ref, xn_ref, sh_ref, sc_ref, wu_ref, wg_ref, cw_ref, cb_ref, wgate_ref,
                   ba_ref, bx_ref, lam_ref, h0_ref,
                   uc_out, sg_out, hf_out,
                   h_scr, u_scr, ucb_scr, pre_scr, ph_scr, prev_scr, carry_scr, *, tm, c):
    i = pl.program_id(1)
    j = pl.program_id(2)
    nt = pl.num_programs(1)

    @pl.when(j == 0)
    def _():
        def body(k, _):
            r0 = pl.multiple_of(k * BF16_ROWS, BF16_ROWS)
            rows = pl.ds(r0, BF16_ROWS)
            h_scr[rows, :] = (x_ref[rows, :] * (1.0 + sc_ref[...]) + sh_ref[...]).astype(BF16)
            return 0
        lax.fori_loop(0, tm // BF16_ROWS, body, 0)
        h_scr[tm:tm + HALO, :] = (xn_ref[...] * (1.0 + sc_ref[...]) + sh_ref[...]).astype(BF16)

    u_scr[GHOST:, :] = jnp.dot(h_scr[...], wu_ref[...], preferred_element_type=F32)
    sg_out[...] = _half_silu(jnp.dot(h_scr[0:tm, :], wg_ref[...], preferred_element_type=F32))

    prev2 = jnp.where(i == 0, 0.0, prev_scr[j])
    nxt = jnp.where(i == nt - 1, 0.0, u_scr[GHOST + tm:GHOST + tm + SUBLANES, :])
    prev_scr[j] = u_scr[tm:tm + GHOST, :]
    _fill_ghost_rows(u_scr, tm, prev2, nxt)

    rb = 32
    def conv_body(k, _):
        r0 = pl.multiple_of(k * rb, rb)
        uc = _conv_rows(u_scr, cw_ref, cb_ref, r0, rb)
        uc_out[pl.ds(r0, rb), :] = uc
        ucb_scr[pl.ds(r0, rb), :] = uc.astype(BF16)
        return 0
    lax.fori_loop(0, tm // rb, conv_body, 0)

    _gate_preacts(lambda kb: ucb_scr[:, kb * LRU_BLOCK:(kb + 1) * LRU_BLOCK],
                  wgate_ref, pre_scr, c, 1)

    p_end, h_end = _local_scan(pre_scr, uc_out, 0, ba_ref[...], bx_ref[...],
                               _decay_rate(lam_ref[...]), 0, c, ph_scr, tm, c, False,
                               SCAN_UNROLL)
    carry = jnp.where(i == 0, h0_ref[...], carry_scr[j])
    start, carry_scr[j] = _chain_starts(p_end, h_end, carry, False)

    def fix_body(k, _):
        rows = pl.ds(pl.multiple_of(k * SUBLANES, SUBLANES), SUBLANES)
        hf_out[rows, :] = ph_scr[rows, c:2 * c] + ph_scr[rows, 0:c] * start
        return 0
    lax.fori_loop(0, tm // SUBLANES, fix_body, 0, unroll=8)


def _l0_forward(x, mod4, wu, wg, conv_w, conv_b, wgate, ba, bx, lam, h0):
    bsz, seq, _ = x.shape
    tm, c = TM0, C0
    nt, nc = seq // tm, D_INNER // c
    last_halo = seq // HALO - 1
    kern = functools.partial(_l0_fwd_kernel, tm=tm, c=c)
    act = jax.ShapeDtypeStruct((bsz, seq, D_INNER), F32)
    chunk = lambda: pl.BlockSpec((1, c), lambda b, i, j: (0, j))
    tile = lambda: pl.BlockSpec((None, tm, c), lambda b, i, j: (b, i, j))
    return pl.pallas_call(
        kern,
        grid=(bsz, nt, nc),
        in_specs=[
            pl.BlockSpec((None, tm, D_MODEL), lambda b, i, j: (b, i, 0)),
            pl.BlockSpec((None, HALO, D_MODEL),
                         lambda b, i, j: (b, jnp.minimum((i + 1) * (tm // HALO), last_halo), 0)),
            pl.BlockSpec((None, None, 1, D_MODEL), lambda b, i, j: (0, b, 0, 0)),
            pl.BlockSpec((None, None, 1, D_MODEL), lambda b, i, j: (0, b, 0, 1)),
            pl.BlockSpec((D_MODEL, c), lambda b, i, j: (0, j)),
            pl.BlockSpec((D_MODEL, c), lambda b, i, j: (0, j)),
            pl.BlockSpec((CONV_W, c), lambda b, i, j: (0, j)),
            chunk(),
            pl.BlockSpec((c // LRU_BLOCK, LRU_BLOCK, 2 * LRU_BLOCK), lambda b, i, j: (j, 0, 0)),
            chunk(), chunk(), chunk(),
            pl.BlockSpec((None, SUBLANES, c), lambda b, i, j: (b, 0, j)),
        ],
        out_specs=[tile(), tile(), tile()],
        out_shape=[act, act, act],
        scratch_shapes=[
            pltpu.VMEM((tm + HALO, D_MODEL), BF16),
            pltpu.VMEM((GHOST + tm + HALO, c), F32),
            pltpu.VMEM((tm, c), BF16),
            pltpu.VMEM((tm, 2 * c), F32),
            pltpu.VMEM((tm, 2 * c), F32),
            pltpu.VMEM((nc, GHOST, c), F32),
            pltpu.VMEM((nc, SUBLANES, c), F32),
        ],
        compiler_params=pltpu.CompilerParams(
            dimension_semantics=("arbitrary", "arbitrary", "arbitrary"),
            vmem_limit_bytes=VMEM_LIMIT),
        name="l0_forward",
    )(x, x, mod4, mod4, wu, wg, conv_w, conv_b, wgate, ba, bx, lam, h0)


def _residual_ln_rows(x_ref, acc_scr, gt_ref, lng_ref, lnb_ref, o_ref, tm):
    def body(k, _):
        r0 = pl.multiple_of(k * SUBLANES, SUBLANES)
        rows = pl.ds(r0, SUBLANES)
        v = ALPHA * x_ref[rows, :] + gt_ref[...] * acc_scr[rows, :]
        mu = jnp.mean(v, axis=-1, keepdims=True)
        d = v - mu
        var = jnp.mean(d * d, axis=-1, keepdims=True)
        o_ref[rows, :] = d * lax.rsqrt(var + LN_EPS) * lng_ref[...] + lnb_ref[...]
        return 0
    lax.fori_loop(0, tm // SUBLANES, body, 0, unroll=8)


def _l0_bwd_kernel(uc_ref, sg_ref, hf_ref, x_ref, gt_ref, wgate_ref, ba_ref, bx_ref, lam_ref,
                   h0_ref, wo_ref, lng_ref, lnb_ref,
                   o_ref,
                   pre_scr, ph_scr, yg_scr, acc_scr, carry_scr, *, tm, c):
    i = pl.program_id(1)
    j = pl.program_id(2)
    nc = pl.num_programs(2)

    _gate_preacts(lambda kb: uc_ref[:, kb * LRU_BLOCK:(kb + 1) * LRU_BLOCK].astype(BF16),
                  wgate_ref, pre_scr, c, 1)

    p_end, h_end = _local_scan(pre_scr, uc_ref, 0, ba_ref[...], bx_ref[...],
                               _decay_rate(lam_ref[...]), 0, c, ph_scr, tm, c, True,
                               SCAN_UNROLL)
    carry = jnp.where(i == 0, h0_ref[...], carry_scr[j])
    start, carry_scr[j] = _chain_starts(p_end, h_end, carry, True)

    start2 = jnp.concatenate([start, start], axis=0)

    def fix_body(k, _):
        rows = pl.ds(pl.multiple_of(k * BF16_ROWS, BF16_ROWS), BF16_ROWS)
        hb = ph_scr[rows, c:2 * c] + ph_scr[rows, 0:c] * start2
        yg_scr[rows, :] = ((hf_ref[rows, :] + hb) * sg_ref[rows, :]).astype(BF16)
        return 0
    lax.fori_loop(0, tm // BF16_ROWS, fix_body, 0, unroll=4)

    @pl.when(j == 0)
    def _():
        acc_scr[...] = jnp.zeros_like(acc_scr)

    acc_scr[...] += jnp.dot(yg_scr[...], wo_ref[...], preferred_element_type=F32)

    @pl.when(j == nc - 1)
    def _():
        _residual_ln_rows(x_ref, acc_scr, gt_ref, lng_ref, lnb_ref, o_ref, tm)


def _l0_backward(uc, sg, hf, x, mod4, wgate, ba, bx, lam, h0, wo, ln_g, ln_b):
    bsz, seq, _ = x.shape
    tm, c = TM0, C0
    nt, nc = seq // tm, D_INNER // c
    kern = functools.partial(_l0_bwd_kernel, tm=tm, c=c)
    chunk = lambda: pl.BlockSpec((1, c), lambda b, i, j: (0, j))
    tile = lambda: pl.BlockSpec((None, tm, c), lambda b, i, j: (b, nt - 1 - i, j))
    full = lambda: pl.BlockSpec((1, D_MODEL), lambda b, i, j: (0, 0))
    return pl.pallas_call(
        kern,
        grid=(bsz, nt, nc),
        in_specs=[
            tile(), tile(), tile(),
            pl.BlockSpec((None, tm, D_MODEL), lambda b, i, j: (b, nt - 1 - i, 0)),
            pl.BlockSpec((None, None, 1, D_MODEL), lambda b, i, j: (0, b, 0, 2)),
            pl.BlockSpec((c // LRU_BLOCK, LRU_BLOCK, 2 * LRU_BLOCK), lambda b, i, j: (j, 0, 0)),
            chunk(), chunk(), chunk(),
            pl.BlockSpec((None, SUBLANES, c), lambda b, i, j: (b, 0, j)),
            pl.BlockSpec((c, D_MODEL), lambda b, i, j: (j, 0)),
            full(), full(),
        ],
        out_specs=pl.BlockSpec((None, tm, D_MODEL), lambda b, i, j: (b, nt - 1 - i, 0)),
        out_shape=jax.ShapeDtypeStruct((bsz, seq, D_MODEL), F32),
        scratch_shapes=[
            pltpu.VMEM((tm, 2 * c), F32),
            pltpu.VMEM((tm, 2 * c), F32),
            pltpu.VMEM((tm, c), BF16),
            pltpu.VMEM((tm, D_MODEL), F32),
            pltpu.VMEM((nc, SUBLANES, c), F32),
        ],
        compiler_params=pltpu.CompilerParams(
            dimension_semantics=("arbitrary", "arbitrary", "arbitrary"),
            vmem_limit_bytes=VMEM_LIMIT),
        name="l0_backward",
    )(uc, sg, hf, x, mod4, wgate, ba, bx, lam, h0, wo, ln_g, ln_b)


def _l1_in_kernel(x_ref, sh_ref, sc_ref, wu_ref, wg_ref, u_out, sg_out, h_scr, *, tm):
    @pl.when(pl.program_id(2) == 0)
    def _():
        def body(k, _):
            rows = pl.ds(pl.multiple_of(k * BF16_ROWS, BF16_ROWS), BF16_ROWS)
            h_scr[rows, :] = (x_ref[rows, :] * (1.0 + sc_ref[...]) + sh_ref[...]).astype(BF16)
            return 0
        lax.fori_loop(0, tm // BF16_ROWS, body, 0)

    u_out[...] = jnp.dot(h_scr[...], wu_ref[...], preferred_element_type=F32)
    sg_out[...] = _half_silu(jnp.dot(h_scr[...], wg_ref[...], preferred_element_type=F32))


def _l1_inproj(x, mod4, wu, wg):
    bsz, seq, _ = x.shape
    tm, c = TM1, POOL_GROUP
    nt, nc = seq // tm, D_INNER // c
    act = jax.ShapeDtypeStruct((bsz, seq, D_INNER), F32)
    tile = lambda: pl.BlockSpec((None, tm, c), lambda b, i, j: (b, i, j))
    return pl.pallas_call(
        functools.partial(_l1_in_kernel, tm=tm),
        grid=(bsz, nt, nc),
        in_specs=[
            pl.BlockSpec((None, tm, D_MODEL), lambda b, i, j: (b, i, 0)),
            pl.BlockSpec((None, None, 1, D_MODEL), lambda b, i, j: (1, b, 0, 0)),
            pl.BlockSpec((None, None, 1, D_MODEL), lambda b, i, j: (1, b, 0, 1)),
            pl.BlockSpec((D_MODEL, c), lambda b, i, j: (0, j)),
            pl.BlockSpec((D_MODEL, c), lambda b, i, j: (0, j)),
        ],
        out_specs=[tile(), tile()],
        out_shape=[act, act],
        scratch_shapes=[pltpu.VMEM((tm, D_MODEL), BF16)],
        compiler_params=pltpu.CompilerParams(
            dimension_semantics=("arbitrary", "arbitrary", "arbitrary"),
            vmem_limit_bytes=VMEM_LIMIT),
        name="l1_inproj",
    )(x, mod4, mod4, wu, wg)


def _shift_rows(x, s):
    n = x.shape[0]
    t = lax.broadcasted_iota(jnp.int32, x.shape, 0)
    rolled = pltpu.roll(x, s % n, 0)
    valid = (t >= s) if s > 0 else (t < n + s)
    return jnp.where(valid, rolled, 0.0)


def _col_window_sum(x, w):
    half = w // 2
    right = x
    left = _shift_rows(x, 1)
    m = 1
    while m < half:
        right = right + _shift_rows(right, -m)
        left = left + _shift_rows(left, m)
        m *= 2
    return left + right


def _window_count(pos, half, n):
    return (jnp.minimum(pos + half, n) - jnp.maximum(pos - half, 0)).astype(F32)


def _pool_group(e_scr, d_scr, w, row0, tm, n_rows):
    half = w // 2
    halo = POOL_HALO
    col = lax.broadcasted_iota(jnp.int32, (GRID_W, LANES), 0)
    inv_c = 1.0 / _window_count(col, half, GRID_W)
    nrow_tile = tm // GRID_W

    def body(r, _):
        base = pl.multiple_of(halo + r * GRID_W, GRID_W)
        grow = jnp.zeros((GRID_W, LANES), jnp.int32) + (row0 + r)
        inv = inv_c / _window_count(grow, half, n_rows)
        for lt in range(POOL_GROUP // LANES):
            lanes = slice(lt * LANES, (lt + 1) * LANES)
            s = e_scr[pl.ds(pl.multiple_of(base - half * GRID_W, GRID_W), GRID_W), lanes]
            for dr in range(-half + 1, half):
                s = s + e_scr[pl.ds(pl.multiple_of(base + dr * GRID_W, GRID_W), GRID_W), lanes]
            s = _col_window_sum(s, w)
            d = s * inv - e_scr[pl.ds(base, GRID_W), lanes]
            d_scr[pl.ds(pl.multiple_of(r * GRID_W, GRID_W), GRID_W), lanes] = d.astype(BF16)
        return 0

    lax.fori_loop(0, nrow_tile, body, 0)


def _l1_pool_kernel(up_ref, u_ref, un_ref, sg_ref, x_ref, gt_ref, pw_ref, ps_ref, wo_ref,
                    lng_ref, lnb_ref, o_ref, e_scr, d_scr, acc_scr, *, tm, n_rows):
    i = pl.program_id(1)
    k = pl.program_id(2)
    nt = pl.num_programs(1)
    halo = POOL_HALO

    e_scr[halo:halo + tm, :] = u_ref[...]

    @pl.when(i == 0)
    def _():
        e_scr[0:halo, :] = jnp.zeros((halo, POOL_GROUP), F32)

    @pl.when(i > 0)
    def _():
        e_scr[0:halo, :] = up_ref[...]

    @pl.when(i == nt - 1)
    def _():
        e_scr[halo + tm:, :] = jnp.zeros((halo, POOL_GROUP), F32)

    @pl.when(i < nt - 1)
    def _():
        e_scr[halo + tm:, :] = un_ref[...]

    row0 = i * (tm // GRID_W)
    for g, w in enumerate(POOL_WINDOWS):
        @pl.when(k == g)
        def _(w=w):
            _pool_group(e_scr, d_scr, w, row0, tm, n_rows)

    y = jnp.dot(d_scr[...], pw_ref[...], preferred_element_type=F32)
    yg = (y * ps_ref[...] * sg_ref[...]).astype(BF16)

    @pl.when(k == 0)
    def _():
        acc_scr[...] = jnp.zeros_like(acc_scr)

    acc_scr[...] += jnp.dot(yg, wo_ref[...], preferred_element_type=F32)

    @pl.when(k == pl.num_programs(2) - 1)
    def _():
        _residual_ln_rows(x_ref, acc_scr, gt_ref, lng_ref, lnb_ref, o_ref, tm)


def _l1_pool(u, sg, x, mod4, pool_w, pool_scale, wo, ln_g, ln_b):
    bsz, seq, _ = x.shape
    tm, c = TM1, POOL_GROUP
    nt, ng = seq // tm, D_INNER // c
    hb = tm // POOL_HALO
    n_halo = seq // POOL_HALO
    kern = functools.partial(_l1_pool_kernel, tm=tm, n_rows=seq // GRID_W)
    full = lambda: pl.BlockSpec((1, D_MODEL), lambda b, i, k: (0, 0))
    return pl.pallas_call(
        kern,
        grid=(bsz, nt, ng),
        in_specs=[
            pl.BlockSpec((None, POOL_HALO, c), lambda b, i, k: (b, jnp.maximum(i * hb - 1, 0), k)),
            pl.BlockSpec((None, tm, c), lambda b, i, k: (b, i, k)),
            pl.BlockSpec((None, POOL_HALO, c),
                         lambda b, i, k: (b, jnp.minimum((i + 1) * hb, n_halo - 1), k)),
            pl.BlockSpec((None, tm, c), lambda b, i, k: (b, i, k)),
            pl.BlockSpec((None, tm, D_MODEL), lambda b, i, k: (b, i, 0)),
            pl.BlockSpec((None, None, 1, D_MODEL), lambda b, i, k: (1, b, 0, 2)),
            pl.BlockSpec((None, c, c), lambda b, i, k: (k, 0, 0)),
            pl.BlockSpec((1, c), lambda b, i, k: (0, k)),
            pl.BlockSpec((c, D_MODEL), lambda b, i, k: (k, 0)),
            full(), full(),
        ],
        out_specs=pl.BlockSpec((None, tm, D_MODEL), lambda b, i, k: (b, i, 0)),
        out_shape=jax.ShapeDtypeStruct((bsz, seq, D_MODEL), F32),
        scratch_shapes=[
            pltpu.VMEM((tm + 2 * POOL_HALO, c), F32),
            pltpu.VMEM((tm, c), BF16),
            pltpu.VMEM((tm, D_MODEL), F32),
        ],
        compiler_params=pltpu.CompilerParams(
            dimension_semantics=("arbitrary", "arbitrary", "arbitrary"),
            vmem_limit_bytes=VMEM_LIMIT),
        name="l1_pool",
    )(u, u, u, sg, x, mod4, pool_w, pool_scale, wo, ln_g, ln_b)


def _interleave_tiles(a, tm):
    b, l, d = a.shape
    return a.reshape(b, l // tm, SUBLANES, tm // SUBLANES, d).swapaxes(2, 3).reshape(b, l, d)


def _deinterleave_tiles(a, tm):
    b, l, d = a.shape
    return a.reshape(b, l // tm, tm // SUBLANES, SUBLANES, d).swapaxes(2, 3).reshape(b, l, d)


def kernel(x, c, ctx, c_ctx, w_mod, b_mod, w_in, w_out, ln_g, ln_b, conv_w, conv_b,
           lru_wa, lru_ba, lru_wx, lru_bx, lru_lam, pool_w, pool_scale):
    bsz = x.shape[0]
    assert bsz + 1 <= SUBLANES
    cc = jnp.zeros((SUBLANES, D_MODEL), F32).at[:bsz].set(c).at[bsz].set(c_ctx)
    mod = _adaln(cc, w_mod, b_mod)
    mod4 = mod.reshape(DEPTH, SUBLANES, 1, 3 * D_MODEL)

    w_out_b = w_out.astype(BF16)
    wu0, wg0 = w_in[0, :, :D_INNER].astype(BF16), (0.5 * w_in[0, :, D_INNER:]).astype(BF16)
    wu1, wg1 = w_in[1, :, :D_INNER].astype(BF16), (0.5 * w_in[1, :, D_INNER:]).astype(BF16)

    wa, wx = (0.5 * lru_wa[0]).astype(BF16), (0.5 * lru_wx[0]).astype(BF16)
    wgate_f = jnp.concatenate([wa[0], wx[0]], axis=-1)
    wgate_b = jnp.concatenate([wa[1], wx[1]], axis=-1)
    wgate_fb = jnp.concatenate([wgate_f, wgate_b], axis=-1)
    ba_h, bx_h, lam = 0.5 * lru_ba[0], 0.5 * lru_bx[0], lru_lam[0]

    x_il = _interleave_tiles(x, TM0)
    ctx_il = _interleave_tiles(ctx, ctx.shape[1])
    h0f, h0b = _ctx_states(ctx_il, mod4, wu0, conv_w[0], conv_b, wgate_fb, ba_h, bx_h, lam)
    uc, sg, hf = _l0_forward(x_il, mod4, wu0, wg0, conv_w[0], conv_b, wgate_f,
                             ba_h[0:1], bx_h[0:1], lam[0:1], h0f)
    x1_il = _l0_backward(uc, sg, hf, x_il, mod4, wgate_b, ba_h[1:2], bx_h[1:2], lam[1:2], h0b,
                         w_out_b[0], ln_g[0:1], ln_b[0:1])
    x1 = _deinterleave_tiles(x1_il, TM0)

    u1, sg1 = _l1_inproj(x1, mod4, wu1, wg1)
    return _l1_pool(u1, sg1, x1, mod4, pool_w[0].astype(BF16), pool_scale, w_out_b[1],
                    ln_g[1:2], ln_b[1:2])
```
